```python
import functools
import jax, jax.numpy as jnp
from jax import lax
import numpy as np

D_MODEL = 2048
BATCH = 2
SEQ = 4096
DEPTH = 4
DEC_BATCH = 8
DEC_SEQ = 4
PAST_LEN = 16384
PAGE_SIZE = 128

POOL_WIDTH = D_MODEL // 2
POOL_WINDOWS = (2, 4, 8, 16)
POOL_GROUPS = len(POOL_WINDOWS)
POOL_GW = POOL_WIDTH // POOL_GROUPS
POOL_HIST = max(POOL_WINDOWS) - 1
N_HEADS = 16
HEAD_DIM = 64
N_KV_HEADS = 4
GROUP = N_HEADS // N_KV_HEADS
ATT_WIDTH = N_HEADS * HEAD_DIM
KV_WIDTH = N_KV_HEADS * HEAD_DIM
ROT_DIM = HEAD_DIM // 4
ROPE_THETA = 500000.0
CMP_STRIDE = 16
CMP_LEN = 2 * CMP_STRIDE
SEL_LEN = 64
N_SEL = 16
WINDOW = 512
Q_BLOCK = 128
SM_SCALE = HEAD_DIM ** -0.5
FORCE_BONUS = 1.0e4
D_FF = 4 * D_MODEL
EPS = 1e-6
_S1 = POOL_WIDTH
_S2 = _S1 + ATT_WIDTH
_S3 = _S2 + 6 * KV_WIDTH
_S4 = _S3 + 3 * N_HEADS
IN_COLS = _S4 + 2 * D_MODEL

kernel_name = 'hybrid_pool_nsa_decoder_step'


def rmsnorm(x, g):
    xf = x.astype(jnp.float32)
    y = xf * lax.rsqrt(jnp.mean(xf * xf, axis=-1, keepdims=True) + EPS)
    return (y * g.astype(jnp.float32)).astype(x.dtype)


def rope(x, pos):
    half = ROT_DIM // 2
    inv = ROPE_THETA ** (-jnp.arange(half, dtype=jnp.float32) / half)
    ang = pos.astype(jnp.float32)[:, None] * inv[None, :]
    shp = (pos.shape[0],) + (1,) * (x.ndim - 3) + (half,)
    cos = jnp.cos(ang).reshape(shp)
    sin = jnp.sin(ang).reshape(shp)
    xf = x.astype(jnp.float32)
    x1 = xf[..., :half]
    x2 = xf[..., half:ROT_DIM]
    out = jnp.concatenate([x1 * cos - x2 * sin, x2 * cos + x1 * sin, xf[..., ROT_DIM:]], axis=-1)
    return out.astype(x.dtype)


def masked_softmax(s, mask, axis=-1):
    s = jnp.where(mask, s.astype(jnp.float32), -jnp.inf)
    m = jnp.max(s, axis=axis, keepdims=True)
    m = jnp.where(jnp.isfinite(m), m, 0.0)
    p = jnp.exp(s - m)
    d = jnp.sum(p, axis=axis, keepdims=True)
    return p / jnp.where(d > 0, d, 1.0)


def modulation(c, w, b):
    mod = jax.nn.silu(c) @ w + b
    return jnp.split(mod[:, None, :], 6, axis=-1)


def modulated_norm(x, g, shift, scale):
    return rmsnorm(x, g) * (1 + scale) + shift


def project_inputs(h, w_in):
    b, t, _ = h.shape
    u, q, kv, g_nsa, g_merge = jnp.split(h @ w_in, [_S1, _S2, _S3, _S4], axis=-1)
    q = q.reshape(b, t, N_KV_HEADS, GROUP, HEAD_DIM)
    kv = kv.reshape(b, t, 6, N_KV_HEADS, HEAD_DIM)
    g_nsa = jax.nn.sigmoid(g_nsa.reshape(b, t, 3, N_KV_HEADS, GROUP))[..., None]
    g_merge = jax.nn.sigmoid(g_merge.reshape(b, t, 2, D_MODEL))
    return u, q, kv, g_nsa, g_merge


def pool_mix(u, hist, pos0, pool_map, pool_scale):
    b, t, _ = u.shape
    ext = jnp.concatenate([hist.astype(u.dtype), u], axis=1)
    cs = jnp.pad(jnp.cumsum(ext.astype(jnp.float32), axis=1), ((0, 0), (1, 0), (0, 0)))
    pos = pos0 + jnp.arange(t)
    hl = POOL_HIST
    groups = []
    for gi, w in enumerate(POOL_WINDOWS):
        sl = slice(gi * POOL_GW, (gi + 1) * POOL_GW)
        win_sum = cs[:, hl + 1:hl + 1 + t, sl] - cs[:, hl + 1 - w:hl + 1 - w + t, sl]
        cnt = jnp.minimum(pos + 1, w).astype(jnp.float32)[None, :, None]
        groups.append(win_sum / cnt - u[..., sl].astype(jnp.float32))
    pooled = jnp.stack(groups, axis=2)
    mixed = jnp.einsum('btgc,gcd->btgd', pooled, pool_map).reshape(b, t, POOL_WIDTH) * pool_scale
    return mixed.astype(u.dtype), ext[:, -hl:]


def compress_blocks(rows, wpos, wmap):
    b, t = rows.shape[:2]
    nsub = t // CMP_STRIDE
    r = rows[:, :nsub * CMP_STRIDE].astype(jnp.float32).reshape(b, nsub, CMP_STRIDE, N_KV_HEADS, HEAD_DIM)
    first = jnp.einsum('bjrhd,rhd->bjhd', r, wpos[:CMP_STRIDE])
    second = jnp.einsum('bjrhd,rhd->bjhd', r, wpos[CMP_STRIDE:])
    blk = (first[:, :-1] + second[:, 1:]) / CMP_LEN
    return jnp.einsum('bjhd,hde->bjhe', blk, wmap).astype(rows.dtype)


def compressed_attention(q, kc, vc, qpos):
    nc = kc.shape[1]
    end = jnp.arange(nc) * CMP_STRIDE + (CMP_LEN - 1)
    s = jnp.einsum('bqhgd,bchd->bhgqc', q, kc).astype(jnp.float32) * SM_SCALE
    p = masked_softmax(s, end[None, :] <= qpos[:, None])
    o = jnp.einsum('bhgqc,bchd->bqhgd', p.astype(vc.dtype), vc)
    return o, p.sum(axis=2)


def select_blocks(p_grp, qpos, n_blk):
    b, h, qn, nc = p_grp.shape
    per = SEL_LEN // CMP_STRIDE
    p = jnp.pad(p_grp, ((0, 0), (0, 0), (0, 0), (0, n_blk * per - nc)))
    score = p.reshape(b, h, qn, n_blk, per).sum(-1)
    j = jnp.arange(n_blk)[None, :]
    cur = (qpos // SEL_LEN)[:, None]
    forced = (j == 0) | (j == cur) | (j == cur - 1)
    score = jnp.where(j <= cur, score + jnp.where(forced, FORCE_BONUS, 0.0), -jnp.inf)
    _, idx = lax.top_k(score, min(N_SEL, n_blk))
    return idx


def to_sel_blocks(rows, n_blk):
    b, t = rows.shape[:2]
    r = jnp.pad(rows, ((0, 0), (0, n_blk * SEL_LEN - t), (0, 0), (0, 0)))
    return r.reshape(b, n_blk, SEL_LEN, N_KV_HEADS, HEAD_DIM).transpose(0, 3, 1, 2, 4)


def selected_attention(q, idx, qpos, kblk, vblk):
    bi = jnp.arange(q.shape[0])[:, None, None, None]
    hi = jnp.arange(N_KV_HEADS)[None, :, None, None]
    ks = kblk[bi, hi, idx]
    vs = vblk[bi, hi, idx]
    s = jnp.einsum('bqhgd,bhqnld->bhgqnl', q, ks).astype(jnp.float32) * SM_SCALE
    kpos = idx[..., None] * SEL_LEN + jnp.arange(SEL_LEN)
    mask = (kpos <= qpos[None, None, :, None, None])[:, :, None]
    p = masked_softmax(s, mask, axis=(-2, -1))
    return jnp.einsum('bhgqnl,bhqnld->bqhgd', p.astype(vs.dtype), vs)


def window_attention_banded(q, k, v):
    b, s = k.shape[:2]
    nq = s // Q_BLOCK
    nb = WINDOW // Q_BLOCK
    pad = ((0, 0), (nb * Q_BLOCK, 0), (0, 0), (0, 0))
    kp = jnp.pad(k, pad).reshape(b, nq + nb, Q_BLOCK, N_KV_HEADS, HEAD_DIM)
    vp = jnp.pad(v, pad).reshape(b, nq + nb, Q_BLOCK, N_KV_HEADS, HEAD_DIM)
    kb = jnp.concatenate([kp[:, i:i + nq] for i in range(nb + 1)], axis=2)
    vb = jnp.concatenate([vp[:, i:i + nq] for i in range(nb + 1)], axis=2)
    qb = q.reshape(b, nq, Q_BLOCK, N_KV_HEADS, GROUP, HEAD_DIM)
    sc = jnp.einsum('bnqhgd,bnkhd->bnhgqk', qb, kb).astype(jnp.float32) * SM_SCALE
    qpos = jnp.arange(s).reshape(nq, Q_BLOCK)
    kpos = (jnp.arange(nq)[:, None] - nb) * Q_BLOCK + jnp.arange((nb + 1) * Q_BLOCK)[None, :]
    diff = qpos[:, :, None] - kpos[:, None, :]
    mask = (diff >= 0) & (diff < WINDOW) & (kpos[:, None, :] >= 0)
    p = masked_softmax(sc, mask[None, :, None, None])
    o = jnp.einsum('bnhgqk,bnkhd->bnqhgd', p.astype(vb.dtype), vb)
    return o.reshape(b, s, N_KV_HEADS, GROUP, HEAD_DIM)


def window_attention_direct(q, k, v, qpos, kpos):
    sc = jnp.einsum('bqhgd,bkhd->bhgqk', q, k).astype(jnp.float32) * SM_SCALE
    diff = qpos[:, None] - kpos[None, :]
    p = masked_softmax(sc, (diff >= 0) & (diff < WINDOW))
    return jnp.einsum('bhgqk,bkhd->bqhgd', p.astype(v.dtype), v)


def merge_branches(pool_out, g_nsa, o_cmp, o_sel, o_win, g_merge, w_br_pool, w_br_nsa, w_out):
    b, t = pool_out.shape[:2]
    nsa = (g_nsa[:, :, 0] * o_cmp + g_nsa[:, :, 1] * o_sel + g_nsa[:, :, 2] * o_win).reshape(b, t, ATT_WIDTH)
    m = g_merge[:, :, 0] * (pool_out @ w_br_pool) + g_merge[:, :, 1] * (nsa @ w_br_nsa)
    return m @ w_out


def token_mixer_prompt(h, w_in, pool_map, pool_scale, cmp_pos, cmp_w, w_br_pool, w_br_nsa, w_out):
    b, s, _ = h.shape
    pos = jnp.arange(s)
    u, q, kv, g_nsa, g_merge = project_inputs(h, w_in)
    pool_out, pool_state = pool_mix(u, jnp.zeros((b, POOL_HIST, POOL_WIDTH), u.dtype), 0, pool_map, pool_scale)
    q_rope = rope(q, pos)
    k_sel = rope(kv[:, :, 2], pos)
    k_win = rope(kv[:, :, 4], pos)
    rows = jnp.stack([kv[:, :, 0], kv[:, :, 1], k_sel, kv[:, :, 3]], axis=2)
    win_rows = jnp.stack([k_win, kv[:, :, 5]], axis=2)
    kc = compress_blocks(kv[:, :, 0], cmp_pos[0], cmp_w[0])
    vc = compress_blocks(kv[:, :, 1], cmp_pos[1], cmp_w[1])
    o_cmp, p_grp = compressed_attention(q, kc, vc, pos)
    n_blk = -(-s // SEL_LEN)
    idx = select_blocks(p_grp, pos, n_blk)
    kblk = to_sel_blocks(k_sel, n_blk)
    vblk = to_sel_blocks(kv[:, :, 3], n_blk)
    nq = s // Q_BLOCK
    qb = q_rope.reshape(b, nq, Q_BLOCK, N_KV_HEADS, GROUP, HEAD_DIM).transpose(1, 0, 2, 3, 4, 5)
    ib = idx.reshape(b, N_KV_HEADS, nq, Q_BLOCK, idx.shape[-1]).transpose(2, 0, 1, 3, 4)
    pb = pos.reshape(nq, Q_BLOCK)
    o_sel = lax.map(lambda a: selected_attention(a[0], a[1], a[2], kblk, vblk), (qb, ib, pb))
    o_sel = o_sel.transpose(1, 0, 2, 3, 4, 5).reshape(b, s, N_KV_HEADS, GROUP, HEAD_DIM)
    o_win = window_attention_banded(q_rope, k_win, kv[:, :, 5])
    out = merge_branches(pool_out, g_nsa, o_cmp, o_sel, o_win, g_merge, w_br_pool, w_br_nsa, w_out)
    return out, (rows, win_rows[:, -min(WINDOW, s):], pool_state)


def token_mixer_sample(h, past_kv, win_buf, pool_hist, page_table, w_in, pool_map, pool_scale, cmp_pos, cmp_w,
                       w_br_pool, w_br_nsa, w_out):
    b, t, _ = h.shape
    past_len = page_table.shape[1] * past_kv.shape[1]
    pos = past_len + jnp.arange(t)
    u, q, kv, g_nsa, g_merge = project_inputs(h, w_in)
    pool_out, pool_state = pool_mix(u, pool_hist, past_len, pool_map, pool_scale)
    q_rope = rope(q, pos)
    k_sel = rope(kv[:, :, 2], pos)
    k_win = rope(kv[:, :, 4], pos)
    rows = jnp.stack([kv[:, :, 0], kv[:, :, 1], k_sel, kv[:, :, 3]], axis=2)
    win_rows = jnp.stack([k_win, kv[:, :, 5]], axis=2)
    past = past_kv[page_table].reshape(b, past_len, 4, N_KV_HEADS, HEAD_DIM)
    full = jnp.concatenate([past, rows.astype(past.dtype)], axis=1)
    tot = past_len + t
    kc = compress_blocks(full[:, :, 0], cmp_pos[0], cmp_w[0])
    vc = compress_blocks(full[:, :, 1], cmp_pos[1], cmp_w[1])
    o_cmp, p_grp = compressed_attention(q, kc, vc, pos)
    n_blk = -(-tot // SEL_LEN)
    idx = select_blocks(p_grp, pos, n_blk)
    o_sel = selected_attention(q_rope, idx, pos, to_sel_blocks(full[:, :, 2], n_blk), to_sel_blocks(full[:, :, 3], n_blk))
    wb = win_buf.shape[1]
    win = jnp.concatenate([win_buf, win_rows.astype(win_buf.dtype)], axis=1)
    kpos = past_len - wb + jnp.arange(wb + t)
    o_win = window_attention_direct(q_rope, win[:, :, 0], win[:, :, 1], pos, kpos)
    out = merge_branches(pool_out, g_nsa, o_cmp, o_sel, o_win, g_merge, w_br_pool, w_br_nsa, w_out)
    return out, (rows, win[:, -wb:], pool_state)


def apply_layer(x, c, mixer, ada_w, ada_b, norm_g, w_up, w_down):
    shift1, scale1, gate1, shift2, scale2, gate2 = modulation(c, ada_w, ada_b)
    out, new_state = mixer(modulated_norm(x, norm_g[0], shift1, scale1))
    x = x + (1 + gate1) * rmsnorm(out, norm_g[1])
    a = jax.nn.relu(modulated_norm(x, norm_g[2], shift2, scale2) @ w_up)
    f = (a * a) @ w_down
    x = x + (1 + gate2) * rmsnorm(f, norm_g[3])
    return x, new_state


def setup_inputs(seed: int = 0) -> dict:
    key = jax.random.key(seed)
    ks = jax.random.split(key, 24)
    nrm = lambda k, shape, s=1.0: s * jax.random.normal(k, shape, jnp.float32)
    n_pages = PAST_LEN // PAGE_SIZE
    n_used = DEC_BATCH * n_pages
    n_pool = n_used + n_used // 4
    wb = min(WINDOW, PAST_LEN)
    page_table = jax.random.permutation(ks[7], n_pool)[:n_used].reshape(DEC_BATCH, n_pages).astype(jnp.int32)
    return {
        'x_prompt': nrm(ks[0], (BATCH, SEQ, D_MODEL)),
        'x_sample': nrm(ks[1], (DEC_BATCH, DEC_SEQ, D_MODEL)),
        'cache_kv': nrm(ks[2], (DEPTH, n_pool, PAGE_SIZE, 4, N_KV_HEADS, HEAD_DIM)),
        'cache_win': nrm(ks[3], (DEPTH, DEC_BATCH, wb, 2, N_KV_HEADS, HEAD_DIM)),
        'state_pool': nrm(ks[4], (DEPTH, DEC_BATCH, POOL_HIST, POOL_WIDTH)),
        'page_table': page_table,
        'c_prompt': nrm(ks[5], (BATCH, D_MODEL)),
        'c_sample': nrm(ks[6], (DEC_BATCH, D_MODEL)),
        'ada_w': nrm(ks[8], (DEPTH, D_MODEL, 6 * D_MODEL), 0.5 * D_MODEL ** -0.5),
        'ada_b': nrm(ks[9], (DEPTH, 6 * D_MODEL), 0.01),
        'norm_g': 1.0 + nrm(ks[10], (DEPTH, 4, D_MODEL), 0.05),
        'w_in': nrm(ks[11], (DEPTH, D_MODEL, IN_COLS), D_MODEL ** -0.5),
        'pool_map': nrm(ks[12], (DEPTH, POOL_GROUPS, POOL_GW, POOL_GW), POOL_GW ** -0.5),
        'pool_scale': 1.0 + nrm(ks[13], (DEPTH, POOL_WIDTH), 0.1),
        'cmp_pos': 1.0 + nrm(ks[14], (DEPTH, 2, CMP_LEN, N_KV_HEADS, HEAD_DIM), 0.1),
        'cmp_w': nrm(ks[15], (DEPTH, 2, N_KV_HEADS, HEAD_DIM, HEAD_DIM), HEAD_DIM ** -0.5),
        'w_br_pool': nrm(ks[16], (DEPTH, POOL_WIDTH, D_MODEL), POOL_WIDTH ** -0.5),
        'w_br_nsa': nrm(ks[17], (DEPTH, ATT_WIDTH, D_MODEL), ATT_WIDTH ** -0.5),
        'w_out': nrm(ks[18], (DEPTH, D_MODEL, D_MODEL), D_MODEL ** -0.5),
        'w_up': nrm(ks[19], (DEPTH, D_MODEL, D_FF), D_MODEL ** -0.5),
        'w_down': nrm(ks[20], (DEPTH, D_FF, D_MODEL), D_FF ** -0.5),
    }


def reference(x_prompt, x_sample, cache_kv, cache_win, state_pool, page_table, c_prompt, c_sample,
              ada_w, ada_b, norm_g, w_in, pool_map, pool_scale, cmp_pos, cmp_w, w_br_pool, w_br_nsa,
              w_out, w_up, w_down):
    xp, xs = x_prompt, x_sample
    kv_p, kv_s, win_p, win_s, pool_p, pool_s = [], [], [], [], [], []
    for l in range(DEPTH):
        mix_w = (w_in[l], pool_map[l], pool_scale[l], cmp_pos[l], cmp_w[l], w_br_pool[l], w_br_nsa[l], w_out[l])
        mix_p = functools.partial(token_mixer_prompt, w_in=mix_w[0], pool_map=mix_w[1], pool_scale=mix_w[2],
                                  cmp_pos=mix_w[3], cmp_w=mix_w[4], w_br_pool=mix_w[5], w_br_nsa=mix_w[6],
                                  w_out=mix_w[7])
        mix_s = functools.partial(token_mixer_sample, past_kv=cache_kv[l], win_buf=cache_win[l],
                                  pool_hist=state_pool[l], page_table=page_table, w_in=mix_w[0],
                                  pool_map=mix_w[1], pool_scale=mix_w[2], cmp_pos=mix_w[3], cmp_w=mix_w[4],
                                  w_br_pool=mix_w[5], w_br_nsa=mix_w[6], w_out=mix_w[7])
        xp, (r_p, w_p, s_p) = apply_layer(xp, c_prompt, mix_p, ada_w[l], ada_b[l], norm_g[l], w_up[l], w_down[l])
        xs, (r_s, w_s, s_s) = apply_layer(xs, c_sample, mix_s, ada_w[l], ada_b[l], norm_g[l], w_up[l], w_down[l])
        kv_p.append(r_p); kv_s.append(r_s)
        win_p.append(w_p); win_s.append(w_s)
        pool_p.append(s_p); pool_s.append(s_s)
    kv_prompt = jnp.stack(kv_p)
    kv_sample = jnp.stack(kv_s)
    win_prompt = jnp.stack(win_p)
    win_sample = jnp.stack(win_s)
    pool_prompt = jnp.stack(pool_p)
    pool_sample = jnp.stack(pool_s)
    return (xp, xs, kv_prompt, kv_sample, win_prompt, win_sample, pool_prompt, pool_sample)
```

```python
import functools

import jax
import jax.numpy as jnp
from jax import lax
from jax.experimental import pallas as pl
from jax.experimental.pallas import tpu as pltpu

F32 = jnp.float32
BF16 = jnp.bfloat16

D_MODEL = 2048
POOL_WIDTH = D_MODEL // 2
POOL_WINDOWS = (2, 4, 8, 16)
POOL_GW = POOL_WIDTH // len(POOL_WINDOWS)
POOL_HIST = max(POOL_WINDOWS) - 1
N_HEADS = 16
HEAD_DIM = 64
N_KV_HEADS = 4
GROUP = N_HEADS // N_KV_HEADS
ATT_WIDTH = N_HEADS * HEAD_DIM
KV_WIDTH = N_KV_HEADS * HEAD_DIM
ROT_DIM = HEAD_DIM // 4
ROPE_THETA = 500000.0
CMP_STRIDE = 16
CMP_LEN = 2 * CMP_STRIDE
SEL_LEN = 64
N_SEL = 16
WINDOW = 512
SM_SCALE = HEAD_DIM ** -0.5
FORCE_BONUS = 1.0e4
D_FF = 4 * D_MODEL
EPS = 1e-6
PER_SEL = SEL_LEN // CMP_STRIDE

C_Q = POOL_WIDTH
C_KV = C_Q + ATT_WIDTH
C_GM = C_KV + 6 * KV_WIDTH
C_GN = C_GM + 2 * D_MODEL
LANES = 128
N_IN = C_GN + N_KV_HEADS * LANES
T_PAD = 8
NEG = -1e30
VMEM_LIMIT = 56 * 1024 * 1024


def _cparams(sem):
    return pltpu.CompilerParams(dimension_semantics=sem, vmem_limit_bytes=VMEM_LIMIT)


def _sigmoid(x):
    return 1.0 / (1.0 + jnp.exp(-x))


def _nt(a, b):
    return lax.dot_general(a, b, (((1,), (1,)), ((), ())), preferred_element_type=F32)


def _softmax_parts(s, valid):
    sm = jnp.where(valid, s, NEG)
    m = jnp.max(sm, axis=-1, keepdims=True)
    p = jnp.where(valid, jnp.exp(sm - m), 0.0)
    return p, m, jnp.sum(p, axis=-1, keepdims=True)


def _safe_div(a, d):
    return a / jnp.where(d > 0, d, 1.0)


def _mod_kernel(c_ref, w_ref, b_ref, o_ref):
    c = c_ref[...]
    a = (c * _sigmoid(c)).astype(BF16)
    o_ref[0] = jnp.dot(a, w_ref[0].astype(BF16), preferred_element_type=F32) + b_ref[0]


def _modulation(c_all, ada_w, ada_b):
    depth, d, n = ada_w.shape
    r = c_all.shape[0]
    tn = 1024
    return pl.pallas_call(
        _mod_kernel,
        out_shape=jax.ShapeDtypeStruct((depth, r, n), F32),
        grid=(depth, n // tn),
        in_specs=[
            pl.BlockSpec((r, d), lambda l, j: (0, 0)),
            pl.BlockSpec((1, d, tn), lambda l, j: (l, 0, j)),
            pl.BlockSpec((1, 1, tn), lambda l, j: (l, 0, j)),
        ],
        out_specs=pl.BlockSpec((1, r, tn), lambda l, j: (l, 0, j)),
        compiler_params=_cparams(("parallel", "parallel")),
    )(c_all, ada_w, ada_b.reshape(depth, 1, n))


def _modnorm(x, g, shift, scale):
    y = x * lax.rsqrt(jnp.mean(x * x, axis=-1, keepdims=True) + EPS) * g
    return y * (1.0 + scale) + shift


def _norm_proj_kernel(x_ref, mod_ref, g_ref, w_ref, o_ref, xn_ref):
    @pl.when(pl.program_id(1) == 0)
    def _():
        xn_ref[...] = _modnorm(x_ref[...], g_ref[...], mod_ref[0, 0], mod_ref[0, 1]).astype(BF16)

    o_ref[...] = jnp.dot(xn_ref[...], w_ref[...], preferred_element_type=F32)


def _norm_proj(x, modt, g, w, tm):
    m, d = x.shape
    n = w.shape[1]
    tn = 1024
    r = modt.shape[2]
    mt = m // modt.shape[0]
    return pl.pallas_call(
        _norm_proj_kernel,
        out_shape=jax.ShapeDtypeStruct((m, n), F32),
        grid=(m // tm, n // tn),
        in_specs=[
            pl.BlockSpec((tm, d), lambda i, j: (i, 0)),
            pl.BlockSpec((1, 6, r, d), lambda i, j: (i * tm // mt, 0, 0, 0)),
            pl.BlockSpec((1, d), lambda i, j: (0, 0)),
            pl.BlockSpec((d, tn), lambda i, j: (0, j)),
        ],
        out_specs=pl.BlockSpec((tm, tn), lambda i, j: (i, j)),
        scratch_shapes=[pltpu.VMEM((tm, d), BF16)],
        compiler_params=_cparams(("parallel", "arbitrary")),
    )(x, modt, g, w)


def _rope(x, c, s_lo, s_hi):
    w = x.shape[1]
    n = w // LANES
    if n > 1:
        c = jnp.concatenate([c] * n, axis=1)
        s_lo = jnp.concatenate([s_lo] * n, axis=1)
        s_hi = jnp.concatenate([s_hi] * n, axis=1)
    half = ROT_DIM // 2
    return x * c + pltpu.roll(x, w - half, 1) * s_lo + pltpu.roll(x, half, 1) * s_hi


def _rope_tables(pos):
    half = ROT_DIM // 2
    inv = ROPE_THETA ** (-jnp.arange(half, dtype=F32) / half)
    ang = pos.astype(F32)[:, None] * inv[None, :]
    cos, sin = jnp.cos(ang), jnp.sin(ang)
    t = pos.shape[0]
    ones = jnp.ones((t, HEAD_DIM - ROT_DIM), F32)
    zeros = jnp.zeros((t, HEAD_DIM - ROT_DIM), F32)
    zh = jnp.zeros((t, half), F32)
    c = jnp.concatenate([cos, cos, ones], axis=1)
    s_lo = jnp.concatenate([-sin, zh, zeros], axis=1)
    s_hi = jnp.concatenate([zh, sin, zeros], axis=1)
    rep = LANES // HEAD_DIM
    return tuple(jnp.concatenate([a] * rep, axis=1) for a in (c, s_lo, s_hi))


def _post_prompt_kernel(q_ref, kv0_ref, kv1_ref, kv2_ref, c_ref, sl_ref, sh_ref,
                        rows_ref, win_ref, qc_ref, qr_ref, ks_ref, vs_ref, kw_ref, vw_ref):
    c, s_lo, s_hi = c_ref[...], sl_ref[...], sh_ref[...]
    q = q_ref[0] * SM_SCALE
    qr = _rope(q, c, s_lo, s_hi)
    for g in range(GROUP):
        for h in range(N_KV_HEADS):
            o = (g * N_KV_HEADS + h) * HEAD_DIM
            qc_ref[0, h, g] = q[:, o:o + HEAD_DIM].astype(BF16)
            qr_ref[0, h, g] = qr[:, o:o + HEAD_DIM].astype(BF16)
    kv1 = kv1_ref[0]
    kv2 = kv2_ref[0]
    k_sel = _rope(kv1[:, :KV_WIDTH], c, s_lo, s_hi)
    v_sel = kv1[:, KV_WIDTH:]
    k_win = _rope(kv2[:, :KV_WIDTH], c, s_lo, s_hi)
    v_win = kv2[:, KV_WIDTH:]
    rows_ref[0] = jnp.concatenate([kv0_ref[0], k_sel, v_sel], axis=1)
    win_ref[0] = jnp.concatenate([k_win, v_win], axis=1)
    for h in range(N_KV_HEADS):
        sl = slice(h * HEAD_DIM, (h + 1) * HEAD_DIM)
        ks_ref[0, h] = k_sel[:, sl].astype(BF16)
        vs_ref[0, h] = v_sel[:, sl].astype(BF16)
        kw_ref[0, h] = k_win[:, sl].astype(BF16)
        vw_ref[0, h] = v_win[:, sl].astype(BF16)


def _post_prompt(p3, tables, tm):
    b, s, _ = p3.shape
    kvb = C_KV // (2 * KV_WIDTH)
    head_sds = jax.ShapeDtypeStruct((b, N_KV_HEADS, s, HEAD_DIM), BF16)
    q_sds = jax.ShapeDtypeStruct((b, N_KV_HEADS, GROUP, s, HEAD_DIM), BF16)
    head_spec = pl.BlockSpec((1, N_KV_HEADS, tm, HEAD_DIM), lambda bi, i: (bi, 0, i, 0))
    q_spec = pl.BlockSpec((1, N_KV_HEADS, GROUP, tm, HEAD_DIM), lambda bi, i: (bi, 0, 0, i, 0))
    tab_spec = pl.BlockSpec((tm, LANES), lambda bi, i: (i, 0))
    return pl.pallas_call(
        _post_prompt_kernel,
        out_shape=(
            jax.ShapeDtypeStruct((b, s, 4 * KV_WIDTH), F32),
            jax.ShapeDtypeStruct((b, s, 2 * KV_WIDTH), F32),
            q_sds, q_sds, head_sds, head_sds, head_sds, head_sds,
        ),
        grid=(b, s // tm),
        in_specs=[
            pl.BlockSpec((1, tm, ATT_WIDTH), lambda bi, i: (bi, i, C_Q // ATT_WIDTH)),
            pl.BlockSpec((1, tm, 2 * KV_WIDTH), lambda bi, i: (bi, i, kvb)),
            pl.BlockSpec((1, tm, 2 * KV_WIDTH), lambda bi, i: (bi, i, kvb + 1)),
            pl.BlockSpec((1, tm, 2 * KV_WIDTH), lambda bi, i: (bi, i, kvb + 2)),
            tab_spec, tab_spec, tab_spec,
        ],
        out_specs=(
            pl.BlockSpec((1, tm, 4 * KV_WIDTH), lambda bi, i: (bi, i, 0)),
            pl.BlockSpec((1, tm, 2 * KV_WIDTH), lambda bi, i: (bi, i, 0)),
            q_spec, q_spec, head_spec, head_spec, head_spec, head_spec,
        ),
        compiler_params=_cparams(("parallel", "parallel")),
    )(p3, p3, p3, p3, *tables)


def _pool_kernel(hist_ref, prev_ref, cur_ref, map_ref, scale_ref, o_ref, *, pos0, tq):
    i = pl.program_id(1)
    cur = cur_ref[0]
    prev = jnp.where(i == 0, hist_ref[0], prev_ref[0])
    ext = jnp.concatenate([prev, cur], axis=0)
    n_prev = prev.shape[0]
    pos = pos0 + i * tq + lax.broadcasted_iota(jnp.int32, (tq, 1), 0)
    outs = []
    for gi, w in enumerate(POOL_WINDOWS):
        e = ext[:, gi * POOL_GW:(gi + 1) * POOL_GW]
        sh = 1
        while sh < w:
            e = e + pltpu.roll(e, sh, 0)
            sh *= 2
        win_sum = e[n_prev:n_prev + tq]
        cnt = jnp.minimum(pos + 1, w).astype(F32)
        pooled = win_sum / cnt - cur[:, gi * POOL_GW:(gi + 1) * POOL_GW]
        outs.append(jnp.dot(pooled.astype(BF16), map_ref[gi], preferred_element_type=F32))
    o_ref[0] = (jnp.concatenate(outs, axis=1) * scale_ref[...]).astype(BF16)


def _pool_mix(hist16, prev_src, p3, pool_map, pool_scale, pos0, tq):
    b, t, _ = p3.shape
    nprev = hist16.shape[1]
    step = tq // nprev if tq >= nprev else 1
    return pl.pallas_call(
        functools.partial(_pool_kernel, pos0=pos0, tq=tq),
        out_shape=jax.ShapeDtypeStruct((b, t, POOL_WIDTH), BF16),
        grid=(b, t // tq),
        in_specs=[
            pl.BlockSpec((1, nprev, POOL_WIDTH), lambda bi, i: (bi, 0, 0)),
            pl.BlockSpec((1, nprev, POOL_WIDTH), lambda bi, i: (bi, jnp.maximum(i * step - 1, 0), 0)),
            pl.BlockSpec((1, tq, POOL_WIDTH), lambda bi, i: (bi, i, 0)),
            pl.BlockSpec((len(POOL_WINDOWS), POOL_GW, POOL_GW), lambda bi, i: (0, 0, 0)),
            pl.BlockSpec((1, POOL_WIDTH), lambda bi, i: (0, 0)),
        ],
        out_specs=pl.BlockSpec((1, tq, POOL_WIDTH), lambda bi, i: (bi, i, 0)),
        compiler_params=_cparams(("parallel", "parallel")),
    )(hist16, prev_src, p3, pool_map, pool_scale)


def _stride_sums(x, wf, ws):
    n = x.shape[0] // CMP_STRIDE
    r = x.reshape(n, CMP_STRIDE, x.shape[1])
    return jnp.sum(r * wf[None], axis=1), jnp.sum(r * ws[None], axis=1)


def _cmp_sums_kernel(x_ref, wf_ref, ws_ref, f_ref, s_ref):
    f, s = _stride_sums(x_ref[0], wf_ref[...], ws_ref[...])
    f_ref[0] = f
    s_ref[0] = s


def _cmp_sums_prompt(rows3, wf, ws, tq):
    b, s, _ = rows3.shape
    w2 = 2 * KV_WIDTH
    sds = jax.ShapeDtypeStruct((b, s // CMP_STRIDE, w2), F32)
    ospec = pl.BlockSpec((1, tq // CMP_STRIDE, w2), lambda bi, i: (bi, i, 0))
    return pl.pallas_call(
        _cmp_sums_kernel,
        out_shape=(sds, sds),
        grid=(b, s // tq),
        in_specs=[
            pl.BlockSpec((1, tq, w2), lambda bi, i: (bi, i, 0)),
            pl.BlockSpec((CMP_STRIDE, w2), lambda bi, i: (0, 0)),
            pl.BlockSpec((CMP_STRIDE, w2), lambda bi, i: (0, 0)),
        ],
        out_specs=(ospec, ospec),
        compiler_params=_cparams(("parallel", "parallel")),
    )(rows3, wf, ws)


def _cmp_sums_paged_kernel(pt_ref, *refs, pg):
    del pt_ref
    wf = refs[pg][...]
    ws = refs[pg + 1][...]
    f_ref, s_ref = refs[pg + 2], refs[pg + 3]
    fs, ss = [], []
    for k in range(pg):
        f, s = _stride_sums(refs[k][0], wf, ws)
        fs.append(f)
        ss.append(s)
    f_ref[0] = jnp.concatenate(fs, axis=0)
    s_ref[0] = jnp.concatenate(ss, axis=0)


def _page_specs(pg, page, width, col_block):
    def spec(k):
        return pl.BlockSpec((1, page, width), lambda bi, s, pt: (pt[bi, s * pg + k], 0, col_block))
    return [spec(k) for k in range(pg)]


def _cmp_sums_paged(cache3, page_table, wf, ws, pg):
    nb, n_pages = page_table.shape
    page = cache3.shape[1]
    w2 = 2 * KV_WIDTH
    per = page // CMP_STRIDE
    sds = jax.ShapeDtypeStruct((nb, n_pages * per, w2), F32)
    ospec = pl.BlockSpec((1, pg * per, w2), lambda bi, s, pt: (bi, s, 0))
    wspec = pl.BlockSpec((CMP_STRIDE, w2), lambda bi, s, pt: (0, 0))
    return pl.pallas_call(
        functools.partial(_cmp_sums_paged_kernel, pg=pg),
        out_shape=(sds, sds),
        grid_spec=pltpu.PrefetchScalarGridSpec(
            num_scalar_prefetch=1,
            grid=(nb, n_pages // pg),
            in_specs=_page_specs(pg, page, w2, 0) + [wspec, wspec],
            out_specs=(ospec, ospec),
        ),
        compiler_params=_cparams(("parallel", "parallel")),
    )(page_table, *([cache3] * pg), wf, ws)


def _cmp_map_kernel(f_ref, s_ref, wk_ref, wv_ref, kc_ref, vc_ref):
    f = f_ref[0]
    n = f.shape[0]
    blk = (f + pltpu.roll(s_ref[0], n - 1, 0)) * (1.0 / CMP_LEN)
    keep = lax.broadcasted_iota(jnp.int32, blk.shape, 0) < n - 1
    blk = jnp.where(keep, blk, 0.0).astype(BF16)
    kc_ref[0] = jnp.dot(blk[:, :KV_WIDTH], wk_ref[...], preferred_element_type=F32)
    vc_ref[0] = jnp.dot(blk[:, KV_WIDTH:], wv_ref[...], preferred_element_type=F32)


def _cmp_map(first, second, wk_bd, wv_bd):
    b, n, w2 = first.shape
    sds = jax.ShapeDtypeStruct((b, n, KV_WIDTH), F32)
    ispec = pl.BlockSpec((1, n, w2), lambda bi: (bi, 0, 0))
    wspec = pl.BlockSpec((KV_WIDTH, KV_WIDTH), lambda bi: (0, 0))
    ospec = pl.BlockSpec((1, n, KV_WIDTH), lambda bi: (bi, 0, 0))
    return pl.pallas_call(
        _cmp_map_kernel,
        out_shape=(sds, sds),
        grid=(b,),
        in_specs=[ispec, ispec, wspec, wspec],
        out_specs=(ospec, ospec),
        compiler_params=_cparams(("parallel",)),
    )(first, second, wk_bd, wv_bd)


def _rank_count_static(sc, n_blk):
    jj = lax.broadcasted_iota(jnp.int32, sc.shape, 0)
    cnt = jnp.zeros(sc.shape, F32)
    for i in range(n_blk):
        row = sc[i:i + 1, :]
        cnt = cnt + jnp.where(row > sc, 1.0, jnp.where((row == sc) & (jj > i), 1.0, 0.0))
    return cnt


def _forced_scores(score_t, cur):
    jj = lax.broadcasted_iota(jnp.int32, score_t.shape, 0)
    forced = (jj == 0) | (jj == cur) | (jj == cur - 1)
    valid = jj <= cur
    sc = jnp.where(valid, score_t + jnp.where(forced, FORCE_BONUS, 0.0), -jnp.inf)
    return sc, valid


def _attn_prompt_kernel(qc_ref, qr_ref, kc_ref, vc_ref, ks_ref, vs_ref, kw_ref, vw_ref, gl_ref, o_ref,
                        acc_ref, m_ref, l_ref, *, tq, tk, n_blk):
    i = pl.program_id(2)
    q0 = i * tq
    rows = GROUP * tq
    nsub = PER_SEL * n_blk

    qc = qc_ref[0, 0].reshape(rows, HEAD_DIM)
    s = _nt(qc, kc_ref[0, 0])
    col = lax.broadcasted_iota(jnp.int32, (rows, nsub), 1)
    qpos = q0 + lax.broadcasted_iota(jnp.int32, (rows, nsub), 0) % tq
    c_orig = (col % n_blk) * PER_SEL + col // n_blk
    p, _, d = _softmax_parts(s, c_orig * CMP_STRIDE + (CMP_LEN - 1) <= qpos)
    p = _safe_div(p, d)
    o_cmp = jnp.dot(p.astype(BF16), vc_ref[0, 0], preferred_element_type=F32)
    p_grp = p[0:tq]
    for g in range(1, GROUP):
        p_grp = p_grp + p[g * tq:(g + 1) * tq]

    if nsub == 2 * LANES:
        t = p_grp[:, :LANES] + p_grp[:, LANES:]
        score = t + pltpu.roll(t, LANES // 2, 1)
    else:
        score = p_grp[:, 0:n_blk]
        for r in range(1, PER_SEL):
            score = score + p_grp[:, r * n_blk:(r + 1) * n_blk]
        score = jnp.concatenate([score, jnp.zeros((tq, LANES - n_blk), F32)], axis=1)
    score_t = score.T[:n_blk]
    cur = (q0 + lax.broadcasted_iota(jnp.int32, (n_blk, tq), 1)) // SEL_LEN
    sc, valid = _forced_scores(score_t, cur)
    cnt = _rank_count_static(sc, n_blk)
    sel_t = jnp.where((cnt < min(N_SEL, n_blk)) & valid, 1.0, 0.0)
    if n_blk < LANES:
        sel_t = jnp.concatenate([sel_t, jnp.zeros((LANES - n_blk, tq), F32)], axis=0)
    sel_q = sel_t.T.astype(BF16)

    qr = qr_ref[0, 0].reshape(rows, HEAD_DIM)
    acc_ref[...] = jnp.zeros(acc_ref.shape, F32)
    m_ref[...] = jnp.full(m_ref.shape, NEG, F32)
    l_ref[...] = jnp.zeros(l_ref.shape, F32)

    def body(kt, carry):
        k0 = pl.multiple_of(kt * tk, tk)
        k = ks_ref[0, 0, pl.ds(k0, tk), :]
        v = vs_ref[0, 0, pl.ds(k0, tk), :]
        sk = _nt(qr, k)
        jrow = lax.broadcasted_iota(jnp.int32, (LANES, tk), 0)
        key = k0 + lax.broadcasted_iota(jnp.int32, (LANES, tk), 1)
        expand = jnp.where(jrow == key // SEL_LEN, 1.0, 0.0).astype(BF16)
        mk = jnp.dot(sel_q, expand, preferred_element_type=F32)
        keyq = k0 + lax.broadcasted_iota(jnp.int32, (tq, tk), 1)
        qp = q0 + lax.broadcasted_iota(jnp.int32, (tq, tk), 0)
        mk = jnp.where(keyq <= qp, mk, 0.0)
        ok = jnp.concatenate([mk] * GROUP, axis=0) > 0.5
        sm = jnp.where(ok, sk, NEG)
        m_old = m_ref[...]
        m_new = jnp.maximum(m_old, jnp.max(sm, axis=-1, keepdims=True))
        alpha = jnp.exp(m_old - m_new)
        pk = jnp.where(ok, jnp.exp(sm - m_new), 0.0)
        l_ref[...] = alpha * l_ref[...] + jnp.sum(pk, axis=-1, keepdims=True)
        acc_ref[...] = alpha * acc_ref[...] + jnp.dot(pk.astype(BF16), v, preferred_element_type=F32)
        m_ref[...] = m_new
        return carry

    lax.fori_loop(0, (q0 + tq + tk - 1) // tk, body, 0)
    o_sel = _safe_div(acc_ref[...], l_ref[...])

    nwb = WINDOW // tq
    w0 = pl.multiple_of(jnp.maximum(i - nwb, 0) * tq, tq)
    wlen = (nwb + 1) * tq
    kw = kw_ref[0, 0, pl.ds(w0, wlen), :]
    vw = vw_ref[0, 0, pl.ds(w0, wlen), :]
    sw = _nt(qr, kw)
    diff = (q0 + lax.broadcasted_iota(jnp.int32, (rows, wlen), 0) % tq
            - (w0 + lax.broadcasted_iota(jnp.int32, (rows, wlen), 1)))
    pw, _, dw = _softmax_parts(sw, (diff >= 0) & (diff < WINDOW))
    o_win = _safe_div(jnp.dot(pw.astype(BF16), vw, preferred_element_type=F32), dw)

    sg = _sigmoid(gl_ref[...])
    outs = []
    for g in range(GROUP):
        sl = slice(g * tq, (g + 1) * tq)
        outs.append(sg[:, g:g + 1] * o_cmp[sl]
                    + sg[:, GROUP + g:GROUP + g + 1] * o_sel[sl]
                    + sg[:, 2 * GROUP + g:2 * GROUP + g + 1] * o_win[sl])
    o_ref[...] = jnp.concatenate(outs, axis=1).astype(BF16)


def _attn_prompt(qc, qr, kc, vc, ks, vs, kw, vw, p2, tq, tk):
    b, _, _, s, _ = qc.shape
    n_blk = s // SEL_LEN
    nq = s // tq
    nsub = kc.shape[2]
    rows = GROUP * tq
    q_spec = pl.BlockSpec((1, 1, GROUP, tq, HEAD_DIM), lambda bi, h, i: (bi, h, 0, i, 0))
    c_spec = pl.BlockSpec((1, 1, nsub, HEAD_DIM), lambda bi, h, i: (bi, h, 0, 0))
    f_spec = pl.BlockSpec((1, 1, s, HEAD_DIM), lambda bi, h, i: (bi, h, 0, 0))
    return pl.pallas_call(
        functools.partial(_attn_prompt_kernel, tq=tq, tk=tk, n_blk=n_blk),
        out_shape=jax.ShapeDtypeStruct((b * s, ATT_WIDTH), BF16),
        grid=(b, N_KV_HEADS, nq),
        in_specs=[
            q_spec, q_spec, c_spec, c_spec, f_spec, f_spec, f_spec, f_spec,
            pl.BlockSpec((tq, LANES), lambda bi, h, i: (bi * nq + i, C_GN // LANES + h)),
        ],
        out_specs=pl.BlockSpec((tq, GROUP * HEAD_DIM), lambda bi, h, i: (bi * nq + i, h)),
        scratch_shapes=[
            pltpu.VMEM((rows, HEAD_DIM), F32),
            pltpu.VMEM((rows, 1), F32),
            pltpu.VMEM((rows, 1), F32),
        ],
        compiler_params=_cparams(("parallel", "parallel", "arbitrary")),
    )(qc, qr, kc, vc, ks, vs, kw, vw, p2)


def _merge_kernel(pool_ref, nsa_ref, wp_ref, wn_ref, g0_ref, g1_ref, o_ref):
    a = jnp.dot(pool_ref[...], wp_ref[...], preferred_element_type=F32)
    c = jnp.dot(nsa_ref[...], wn_ref[...], preferred_element_type=F32)
    o_ref[...] = (_sigmoid(g0_ref[...]) * a + _sigmoid(g1_ref[...]) * c).astype(BF16)


def _merge(pool_out, nsa, wp, wn, p2, tm):
    m = pool_out.shape[0]
    tn = 512
    gb = C_GM // tn
    return pl.pallas_call(
        _merge_kernel,
        out_shape=jax.ShapeDtypeStruct((m, D_MODEL), BF16),
        grid=(m // tm, D_MODEL // tn),
        in_specs=[
            pl.BlockSpec((tm, POOL_WIDTH), lambda i, j: (i, 0)),
            pl.BlockSpec((tm, ATT_WIDTH), lambda i, j: (i, 0)),
            pl.BlockSpec((POOL_WIDTH, tn), lambda i, j: (0, j)),
            pl.BlockSpec((ATT_WIDTH, tn), lambda i, j: (0, j)),
            pl.BlockSpec((tm, tn), lambda i, j: (i, gb + j)),
            pl.BlockSpec((tm, tn), lambda i, j: (i, gb + D_MODEL // tn + j)),
        ],
        out_specs=pl.BlockSpec((tm, tn), lambda i, j: (i, j)),
        compiler_params=_cparams(("parallel", "parallel")),
    )(pool_out, nsa, wp, wn, p2, p2)


def _rms(x, g):
    return x * lax.rsqrt(jnp.mean(x * x, axis=-1, keepdims=True) + EPS) * g


def _out_proj_kernel(m_ref, w_ref, x_ref, mod_ref, g_ref, o_ref):
    out = jnp.dot(m_ref[...], w_ref[...], preferred_element_type=F32)
    o_ref[...] = x_ref[...] + (1.0 + mod_ref[0, 2]) * _rms(out, g_ref[...])


def _out_proj(mm, w_out, x, modt, g, tm):
    m, d = x.shape
    r = modt.shape[2]
    mt = m // modt.shape[0]
    return pl.pallas_call(
        _out_proj_kernel,
        out_shape=jax.ShapeDtypeStruct((m, d), F32),
        grid=(m // tm,),
        in_specs=[
            pl.BlockSpec((tm, d), lambda i: (i, 0)),
            pl.BlockSpec((d, d), lambda i: (0, 0)),
            pl.BlockSpec((tm, d), lambda i: (i, 0)),
            pl.BlockSpec((1, 6, r, d), lambda i: (i * tm // mt, 0, 0, 0)),
            pl.BlockSpec((1, d), lambda i: (0, 0)),
        ],
        out_specs=pl.BlockSpec((tm, d), lambda i: (i, 0)),
        compiler_params=_cparams(("parallel",)),
    )(mm, w_out, x, modt, g)


def _mlp_kernel(x_ref, mod_ref, g2_ref, g3_ref, wu_ref, wd_ref, o_ref, xn_ref, acc_ref):
    f = pl.program_id(1)

    @pl.when(f == 0)
    def _():
        xn_ref[...] = _modnorm(x_ref[...], g2_ref[...], mod_ref[0, 3], mod_ref[0, 4]).astype(BF16)
        acc_ref[...] = jnp.zeros(acc_ref.shape, F32)

    a = jnp.maximum(jnp.dot(xn_ref[...], wu_ref[...], preferred_element_type=F32), 0.0)
    acc_ref[...] += jnp.dot((a * a).astype(BF16), wd_ref[...], preferred_element_type=F32)

    @pl.when(f == pl.num_programs(1) - 1)
    def _():
        o_ref[...] = x_ref[...] + (1.0 + mod_ref[0, 5]) * _rms(acc_ref[...], g3_ref[...])


def _mlp(x, modt, g2, g3, w_up, w_down, tm):
    m, d = x.shape
    ff = w_up.shape[1]
    tf = 512
    r = modt.shape[2]
    mt = m // modt.shape[0]
    return pl.pallas_call(
        _mlp_kernel,
        out_shape=jax.ShapeDtypeStruct((m, d), F32),
        grid=(m // tm, ff // tf),
        in_specs=[
            pl.BlockSpec((tm, d), lambda i, f: (i, 0)),
            pl.BlockSpec((1, 6, r, d), lambda i, f: (i * tm // mt, 0, 0, 0)),
            pl.BlockSpec((1, d), lambda i, f: (0, 0)),
            pl.BlockSpec((1, d), lambda i, f: (0, 0)),
            pl.BlockSpec((d, tf), lambda i, f: (0, f)),
            pl.BlockSpec((tf, d), lambda i, f: (f, 0)),
        ],
        out_specs=pl.BlockSpec((tm, d), lambda i, f: (i, 0)),
        scratch_shapes=[pltpu.VMEM((tm, d), BF16), pltpu.VMEM((tm, d), F32)],
        compiler_params=_cparams(("parallel", "arbitrary")),
    )(x, modt, g2, g3, w_up, w_down)


def _head_mask(rows_per_head, n_rep):
    shape = (n_rep * N_KV_HEADS * rows_per_head, KV_WIDTH)
    r = lax.broadcasted_iota(jnp.int32, shape, 0)
    c = lax.broadcasted_iota(jnp.int32, shape, 1)
    return (r // rows_per_head) % N_KV_HEADS == c // HEAD_DIM


def _post_sample_kernel(q_ref, kv0_ref, kv1_ref, kv2_ref, c_ref, sl_ref, sh_ref,
                        rows_ref, win_ref, qbc_ref, qbr_ref):
    c, s_lo, s_hi = c_ref[...], sl_ref[...], sh_ref[...]
    q = q_ref[0] * SM_SCALE
    qr = _rope(q, c, s_lo, s_hi)
    hm = _head_mask(T_PAD, GROUP)

    def stack(x):
        parts = []
        for g in range(GROUP):
            chunk = x[:, g * KV_WIDTH:(g + 1) * KV_WIDTH]
            parts.extend([chunk] * N_KV_HEADS)
        return jnp.where(hm, jnp.concatenate(parts, axis=0), 0.0).astype(BF16)

    qbc_ref[0] = stack(q)
    qbr_ref[0] = stack(qr)
    kv1 = kv1_ref[0]
    kv2 = kv2_ref[0]
    rows_ref[0] = jnp.concatenate(
        [kv0_ref[0], _rope(kv1[:, :KV_WIDTH], c, s_lo, s_hi), kv1[:, KV_WIDTH:]], axis=1)
    win_ref[0] = jnp.concatenate([_rope(kv2[:, :KV_WIDTH], c, s_lo, s_hi), kv2[:, KV_WIDTH:]], axis=1)


def _post_sample(p3, tables):
    b, t, _ = p3.shape
    kvb = C_KV // (2 * KV_WIDTH)
    rq = GROUP * N_KV_HEADS * t
    qb_sds = jax.ShapeDtypeStruct((b, rq, KV_WIDTH), BF16)
    qb_spec = pl.BlockSpec((1, rq, KV_WIDTH), lambda bi: (bi, 0, 0))
    tab_spec = pl.BlockSpec((t, LANES), lambda bi: (0, 0))
    return pl.pallas_call(
        _post_sample_kernel,
        out_shape=(
            jax.ShapeDtypeStruct((b, t, 4 * KV_WIDTH), F32),
            jax.ShapeDtypeStruct((b, t, 2 * KV_WIDTH), F32),
            qb_sds, qb_sds,
        ),
        grid=(b,),
        in_specs=[
            pl.BlockSpec((1, t, ATT_WIDTH), lambda bi: (bi, 0, C_Q // ATT_WIDTH)),
            pl.BlockSpec((1, t, 2 * KV_WIDTH), lambda bi: (bi, 0, kvb)),
            pl.BlockSpec((1, t, 2 * KV_WIDTH), lambda bi: (bi, 0, kvb + 1)),
            pl.BlockSpec((1, t, 2 * KV_WIDTH), lambda bi: (bi, 0, kvb + 2)),
            tab_spec, tab_spec, tab_spec,
        ],
        out_specs=(
            pl.BlockSpec((1, t, 4 * KV_WIDTH), lambda bi: (bi, 0, 0)),
            pl.BlockSpec((1, t, 2 * KV_WIDTH), lambda bi: (bi, 0, 0)),
            qb_spec, qb_spec,
        ),
        compiler_params=_cparams(("parallel",)),
    )(p3, p3, p3, p3, *tables)


def _attn_sample_cw_kernel(qbc_ref, qbr_ref, kc_ref, vc_ref, cw_ref, nw_ref,
                           ocmp_ref, owin_ref, score_ref, *, past_len, n_past_blk):
    rows = qbc_ref.shape[1]
    hrows = N_KV_HEADS * T_PAD
    nsub = kc_ref.shape[1]

    s = _nt(qbc_ref[0], kc_ref[0])
    col = lax.broadcasted_iota(jnp.int32, (rows, nsub), 1)
    qpos = past_len + lax.broadcasted_iota(jnp.int32, (rows, nsub), 0) % T_PAD
    c_orig = (col % n_past_blk) * PER_SEL + col // n_past_blk
    p, _, d = _softmax_parts(s, c_orig * CMP_STRIDE + (CMP_LEN - 1) <= qpos)
    p = _safe_div(p, d)
    ocmp_ref[0] = jnp.dot(p.astype(BF16), vc_ref[0], preferred_element_type=F32)
    p_grp = p[0:hrows]
    for g in range(1, GROUP):
        p_grp = p_grp + p[g * hrows:(g + 1) * hrows]
    score = p_grp[:, 0:n_past_blk]
    for r in range(1, PER_SEL):
        score = score + p_grp[:, r * n_past_blk:(r + 1) * n_past_blk]
    score_ref[0] = score

    qbr = qbr_ref[0]
    cw = cw_ref[0]
    wb = cw.shape[0]
    nw = nw_ref[0]
    n_new = nw.shape[0]
    s1 = _nt(qbr, cw[:, :KV_WIDTH].astype(BF16))
    s2 = _nt(qbr, nw[:, :KV_WIDTH].astype(BF16))
    t1 = lax.broadcasted_iota(jnp.int32, (rows, wb), 0) % T_PAD
    d1 = t1 + wb - lax.broadcasted_iota(jnp.int32, (rows, wb), 1)
    t2 = lax.broadcasted_iota(jnp.int32, (rows, n_new), 0) % T_PAD
    d2 = t2 - lax.broadcasted_iota(jnp.int32, (rows, n_new), 1)
    ok1 = (d1 >= 0) & (d1 < WINDOW)
    ok2 = (d2 >= 0) & (d2 < WINDOW)
    m = jnp.maximum(jnp.max(jnp.where(ok1, s1, NEG), axis=-1, keepdims=True),
                    jnp.max(jnp.where(ok2, s2, NEG), axis=-1, keepdims=True))
    p1 = jnp.where(ok1, jnp.exp(jnp.where(ok1, s1, NEG) - m), 0.0)
    p2 = jnp.where(ok2, jnp.exp(jnp.where(ok2, s2, NEG) - m), 0.0)
    den = jnp.sum(p1, axis=-1, keepdims=True) + jnp.sum(p2, axis=-1, keepdims=True)
    o = (jnp.dot(p1.astype(BF16), cw[:, KV_WIDTH:].astype(BF16), preferred_element_type=F32)
         + jnp.dot(p2.astype(BF16), nw[:, KV_WIDTH:].astype(BF16), preferred_element_type=F32))
    owin_ref[0] = _safe_div(o, den)


def _attn_sample_cw(qbc, qbr, kc, vc, cache_win3, new_win, past_len):
    b, rows, _ = qbc.shape
    nsub = kc.shape[1]
    n_past_blk = nsub // PER_SEL
    wb = cache_win3.shape[1]
    n_new = new_win.shape[1]
    hrows = N_KV_HEADS * T_PAD
    o_sds = jax.ShapeDtypeStruct((b, rows, KV_WIDTH), F32)
    o_spec = pl.BlockSpec((1, rows, KV_WIDTH), lambda bi: (bi, 0, 0))
    return pl.pallas_call(
        functools.partial(_attn_sample_cw_kernel, past_len=past_len, n_past_blk=n_past_blk),
        out_shape=(o_sds, o_sds, jax.ShapeDtypeStruct((b, hrows, n_past_blk), F32)),
        grid=(b,),
        in_specs=[
            o_spec, o_spec,
            pl.BlockSpec((1, nsub, KV_WIDTH), lambda bi: (bi, 0, 0)),
            pl.BlockSpec((1, nsub, KV_WIDTH), lambda bi: (bi, 0, 0)),
            pl.BlockSpec((1, wb, 2 * KV_WIDTH), lambda bi: (bi, 0, 0)),
            pl.BlockSpec((1, n_new, 2 * KV_WIDTH), lambda bi: (bi, 0, 0)),
        ],
        out_specs=(o_spec, o_spec, pl.BlockSpec((1, hrows, n_past_blk), lambda bi: (bi, 0, 0))),
        compiler_params=_cparams(("parallel",)),
    )(qbc, qbr, kc, vc, cache_win3, new_win)


def _rank_sample_kernel(score_ref, sel_ref, sc_ref, cnt_ref, *, cur, n_keep):
    score_t = score_ref[...].T
    n_blk = score_t.shape[0]
    sc, valid = _forced_scores(score_t, cur)
    sc_ref[...] = sc
    cnt_ref[...] = jnp.zeros(cnt_ref.shape, F32)
    jj = lax.broadcasted_iota(jnp.int32, sc.shape, 0)

    def body(i, carry):
        row = sc_ref[pl.ds(i, 1), :]
        scv = sc_ref[...]
        cnt_ref[...] += jnp.where(row > scv, 1.0, jnp.where((row == scv) & (jj > i), 1.0, 0.0))
        return carry

    lax.fori_loop(0, n_blk, body, 0)
    sel = jnp.where((cnt_ref[...] < n_keep) & valid, 1.0, 0.0)
    sel_ref[...] = sel.T.astype(BF16)


def _rank_sample(score2, cur, n_keep):
    rows, n_blk = score2.shape
    return pl.pallas_call(
        functools.partial(_rank_sample_kernel, cur=cur, n_keep=n_keep),
        out_shape=jax.ShapeDtypeStruct((rows, n_blk), BF16),
        grid=(1,),
        in_specs=[pl.BlockSpec((rows, n_blk), lambda i: (0, 0))],
        out_specs=pl.BlockSpec((rows, n_blk), lambda i: (0, 0)),
        scratch_shapes=[pltpu.VMEM((n_blk, rows), F32), pltpu.VMEM((n_blk, rows), F32)],
        compiler_params=_cparams(("arbitrary",)),
    )(score2)


def _attn_sample_sel_kernel(pt_ref, *refs, pg, page):
    del pt_ref
    pages = refs[:pg]
    (qbr_ref, sel_ref, nr_ref, ocmp_ref, owin_ref, gl_ref, o_ref, acc_ref, m_ref, l_ref) = refs[pg:]
    step = pl.program_id(1)
    qbr = qbr_ref[0]
    rows = qbr.shape[0]
    sel = sel_ref[0]
    n_blk = sel.shape[1]

    @pl.when(step == 0)
    def _():
        acc_ref[...] = jnp.zeros(acc_ref.shape, F32)
        m_ref[...] = jnp.full(m_ref.shape, NEG, F32)
        l_ref[...] = jnp.zeros(l_ref.shape, F32)

    def update(sk, ok, v):
        sm = jnp.where(ok, sk, NEG)
        m_old = m_ref[...]
        m_new = jnp.maximum(m_old, jnp.max(sm, axis=-1, keepdims=True))
        alpha = jnp.exp(m_old - m_new)
        pk = jnp.where(ok, jnp.exp(sm - m_new), 0.0)
        l_ref[...] = alpha * l_ref[...] + jnp.sum(pk, axis=-1, keepdims=True)
        acc_ref[...] = alpha * acc_ref[...] + jnp.dot(pk.astype(BF16), v, preferred_element_type=F32)
        m_ref[...] = m_new

    for k in range(pg):
        blk = pages[k][0]
        first_blk = (step * pg + k) * (page // SEL_LEN)
        jrow = lax.broadcasted_iota(jnp.int32, (n_blk, page), 0)
        kcol = lax.broadcasted_iota(jnp.int32, (n_blk, page), 1)
        expand = jnp.where(jrow == first_blk + kcol // SEL_LEN, 1.0, 0.0).astype(BF16)
        mk = jnp.dot(sel, expand, preferred_element_type=F32)
        ok = jnp.concatenate([mk] * GROUP, axis=0) > 0.5
        update(_nt(qbr, blk[:, :KV_WIDTH].astype(BF16)), ok, blk[:, KV_WIDTH:].astype(BF16))

    @pl.when(step == pl.num_programs(1) - 1)
    def _():
        nr = nr_ref[0]
        n_new = nr.shape[0]
        tq = lax.broadcasted_iota(jnp.int32, (rows, n_new), 0) % T_PAD
        tk = lax.broadcasted_iota(jnp.int32, (rows, n_new), 1)
        update(_nt(qbr, nr[:, :KV_WIDTH].astype(BF16)), tk <= tq, nr[:, KV_WIDTH:].astype(BF16))
        o_sel = _safe_div(acc_ref[...], l_ref[...])
        o_cmp = ocmp_ref[0]
        o_win = owin_ref[0]
        sg = _sigmoid(gl_ref[0])
        hcol = lax.broadcasted_iota(jnp.int32, (T_PAD, KV_WIDTH), 1) // HEAD_DIM
        outs = []
        for g in range(GROUP):
            og = jnp.zeros((T_PAD, KV_WIDTH), F32)
            for h in range(N_KV_HEADS):
                r0 = (g * N_KV_HEADS + h) * T_PAD
                lane = h * LANES + g
                mix = (sg[:, lane:lane + 1] * o_cmp[r0:r0 + T_PAD]
                       + sg[:, lane + GROUP:lane + GROUP + 1] * o_sel[r0:r0 + T_PAD]
                       + sg[:, lane + 2 * GROUP:lane + 2 * GROUP + 1] * o_win[r0:r0 + T_PAD])
                og = og + jnp.where(hcol == h, mix, 0.0)
            outs.append(og)
        o_ref[0] = jnp.concatenate(outs, axis=1).astype(BF16)


def _attn_sample_sel(page_table, cache3, qbr, sel3, new_rows, o_cmp, o_win, p3, pg):
    nb, n_pages = page_table.shape
    page = cache3.shape[1]
    rows = qbr.shape[1]
    hrows, n_blk = sel3.shape[1:]
    n_new = new_rows.shape[1]
    w2 = 2 * KV_WIDTH
    gw = N_KV_HEADS * LANES
    full = lambda r, w: pl.BlockSpec((1, r, w), lambda bi, s, pt: (bi, 0, 0))
    return pl.pallas_call(
        functools.partial(_attn_sample_sel_kernel, pg=pg, page=page),
        out_shape=jax.ShapeDtypeStruct((nb, T_PAD, ATT_WIDTH), BF16),
        grid_spec=pltpu.PrefetchScalarGridSpec(
            num_scalar_prefetch=1,
            grid=(nb, n_pages // pg),
            in_specs=_page_specs(pg, page, w2, 1) + [
                full(rows, KV_WIDTH), full(hrows, n_blk), full(n_new, w2),
                full(rows, KV_WIDTH), full(rows, KV_WIDTH),
                pl.BlockSpec((1, T_PAD, gw), lambda bi, s, pt: (bi, 0, C_GN // gw)),
            ],
            out_specs=full(T_PAD, ATT_WIDTH),
            scratch_shapes=[
                pltpu.VMEM((rows, KV_WIDTH), F32),
                pltpu.VMEM((rows, 1), F32),
                pltpu.VMEM((rows, 1), F32),
            ],
        ),
        compiler_params=_cparams(("parallel", "arbitrary")),
    )(page_table, *([cache3] * pg), qbr, sel3, new_rows, o_cmp, o_win, p3)


def _prep_weights(w_in, pool_map, pool_scale, cmp_pos, cmp_w, w_br_pool, w_br_nsa, w_out, w_up, w_down):
    depth = w_in.shape[0]
    s1, s2, s3, s4 = C_Q, C_KV, C_KV + 6 * KV_WIDTH, C_KV + 6 * KV_WIDTH + 3 * N_HEADS
    w_u = w_in[:, :, :s1]
    w_q = w_in[:, :, s1:s2].reshape(depth, D_MODEL, N_KV_HEADS, GROUP, HEAD_DIM)
    w_q = w_q.transpose(0, 1, 3, 2, 4).reshape(depth, D_MODEL, ATT_WIDTH)
    w_kv = w_in[:, :, s2:s3]
    w_gn = w_in[:, :, s3:s4].reshape(depth, D_MODEL, 3, N_KV_HEADS, GROUP)
    w_gn = w_gn.transpose(0, 1, 3, 2, 4).reshape(depth, D_MODEL, N_KV_HEADS, 3 * GROUP)
    w_gn = jnp.pad(w_gn, ((0, 0), (0, 0), (0, 0), (0, LANES - 3 * GROUP)))
    w_gn = w_gn.reshape(depth, D_MODEL, N_KV_HEADS * LANES)
    w_gm = w_in[:, :, s4:]
    w1 = jnp.concatenate([w_u, w_q, w_kv, w_gm, w_gn], axis=2).astype(BF16)
    wn_p = w_br_nsa.astype(BF16)
    wn_s = w_br_nsa.reshape(depth, N_KV_HEADS, GROUP, HEAD_DIM, D_MODEL)
    wn_s = wn_s.transpose(0, 2, 1, 3, 4).reshape(depth, ATT_WIDTH, D_MODEL).astype(BF16)
    cp = cmp_pos.reshape(depth, 2, CMP_LEN, KV_WIDTH)
    wf = jnp.concatenate([cp[:, 0, :CMP_STRIDE], cp[:, 1, :CMP_STRIDE]], axis=-1)
    ws = jnp.concatenate([cp[:, 0, CMP_STRIDE:], cp[:, 1, CMP_STRIDE:]], axis=-1)
    eye = jnp.eye(N_KV_HEADS, dtype=F32)
    bd = jnp.einsum('lshde,hg->lshdge', cmp_w, eye).reshape(depth, 2, KV_WIDTH, KV_WIDTH).astype(BF16)
    return dict(
        w1=w1, wn_p=wn_p, wn_s=wn_s, wp=w_br_pool.astype(BF16), wo=w_out.astype(BF16),
        wu=w_up.astype(BF16), wd=w_down.astype(BF16), pmap=pool_map.astype(BF16),
        pscale=pool_scale.reshape(depth, 1, POOL_WIDTH), wf=wf, ws=ws, wk_bd=bd[:, 0], wv_bd=bd[:, 1])


def _permute_cmp(x, n_blk):
    b, n, w = x.shape
    return x.reshape(b, n_blk, PER_SEL, w).transpose(0, 2, 1, 3).reshape(b, n, w)


def _layer_prompt(x, modt, ng, w, l, tables, b, s, tm):
    m = b * s
    p2 = _norm_proj(x, modt, ng[0:1], w['w1'][l], tm)
    p3 = p2.reshape(b, s, N_IN)
    rows3, win3, qc, qr, ks, vs, kw, vw = _post_prompt(p3, tables, min(512, s))
    hist = jnp.zeros((b, 16, POOL_WIDTH), F32)
    pool_out = _pool_mix(hist, p3, p3, w['pmap'][l], w['pscale'][l], 0, min(512, s))
    first, second = _cmp_sums_prompt(rows3, w['wf'][l], w['ws'][l], min(1024, s))
    kc, vc = _cmp_map(first, second, w['wk_bd'][l], w['wv_bd'][l])
    n_blk = s // SEL_LEN

    def heads(a):
        a = _permute_cmp(a, n_blk).astype(BF16)
        return a.reshape(b, PER_SEL * n_blk, N_KV_HEADS, HEAD_DIM).transpose(0, 2, 1, 3)

    nsa = _attn_prompt(qc, qr, heads(kc), heads(vc), ks, vs, kw, vw, p2, 128, min(512, s))
    mm = _merge(pool_out.reshape(m, POOL_WIDTH), nsa, w['wp'][l], w['wn_p'][l], p2, tm)
    x1 = _out_proj(mm, w['wo'][l], x, modt, ng[1:2], min(512, tm))
    x2 = _mlp(x1, modt, ng[2:3], ng[3:4], w['wu'][l], w['wd'][l], min(512, tm))
    wlen = min(WINDOW, s)
    new_state = (
        rows3.reshape(b, s, 4, N_KV_HEADS, HEAD_DIM),
        win3[:, s - wlen:].reshape(b, wlen, 2, N_KV_HEADS, HEAD_DIM),
        p3[:, s - POOL_HIST:, :POOL_WIDTH],
    )
    return x2, new_state


def _layer_sample(x, modt, ng, w, l, tables, cache3, cache_win3, hist, page_table, nb, t_real):
    m = nb * T_PAD
    n_pages = page_table.shape[1]
    page = cache3.shape[1]
    past_len = n_pages * page
    n_past_blk = past_len // SEL_LEN
    pg = 8 if n_pages % 8 == 0 else 1
    p2 = _norm_proj(x, modt, ng[0:1], w['w1'][l], m)
    p3 = p2.reshape(nb, T_PAD, N_IN)
    rows3, win3, qbc, qbr = _post_sample(p3, tables)
    hist16 = jnp.pad(hist, ((0, 0), (16 - POOL_HIST, 0), (0, 0)))
    pool_out = _pool_mix(hist16, hist16, p3, w['pmap'][l], w['pscale'][l], past_len, T_PAD)
    first, second = _cmp_sums_paged(cache3, page_table, w['wf'][l], w['ws'][l], pg)
    kc, vc = _cmp_map(first, second, w['wk_bd'][l], w['wv_bd'][l])
    kc = _permute_cmp(kc, n_past_blk).astype(BF16)
    vc = _permute_cmp(vc, n_past_blk).astype(BF16)
    n_new = LANES
    tok = (jnp.arange(T_PAD) < t_real)[None, :, None]
    new_win = jnp.pad(jnp.where(tok, win3, 0.0), ((0, 0), (0, n_new - T_PAD), (0, 0)))
    new_rows = jnp.pad(jnp.where(tok, rows3[:, :, 2 * KV_WIDTH:], 0.0), ((0, 0), (0, n_new - T_PAD), (0, 0)))
    o_cmp, o_win, score = _attn_sample_cw(qbc, qbr, kc, vc, cache_win3, new_win, past_len)
    hrows = N_KV_HEADS * T_PAD
    sel = _rank_sample(score.reshape(nb * hrows, n_past_blk), n_past_blk, N_SEL - 1)
    nsa = _attn_sample_sel(page_table, cache3, qbr, sel.reshape(nb, hrows, n_past_blk), new_rows,
                           o_cmp, o_win, p3, pg)
    mm = _merge(pool_out.reshape(m, POOL_WIDTH), nsa.reshape(m, ATT_WIDTH), w['wp'][l], w['wn_s'][l], p2, m)
    x1 = _out_proj(mm, w['wo'][l], x, modt, ng[1:2], m)
    x2 = _mlp(x1, modt, ng[2:3], ng[3:4], w['wu'][l], w['wd'][l], m)
    wb = cache_win3.shape[1]
    u_new = p3[:, :t_real, :POOL_WIDTH]
    new_state = (
        rows3[:, :t_real].reshape(nb, t_real, 4, N_KV_HEADS, HEAD_DIM),
        jnp.concatenate([cache_win3, win3[:, :t_real]], axis=1)[:, -wb:].reshape(nb, wb, 2, N_KV_HEADS, HEAD_DIM),
        jnp.concatenate([hist, u_new], axis=1)[:, -POOL_HIST:],
    )
    return x2, new_state


def kernel(x_prompt, x_sample, cache_kv, cache_win, state_pool, page_table, c_prompt, c_sample, ada_w, ada_b,
           norm_g, w_in, pool_map, pool_scale, cmp_pos, cmp_w, w_br_pool, w_br_nsa, w_out, w_up, w_down):
    depth = w_in.shape[0]
    b, s, d = x_prompt.shape
    nb, t_real, _ = x_sample.shape
    n_pool, page = cache_kv.shape[1:3]
    n_pages = page_table.shape[1]
    past_len = n_pages * page
    wb = cache_win.shape[2]
    assert t_real <= T_PAD and t_real < CMP_STRIDE and past_len % SEL_LEN == 0 and page % SEL_LEN == 0
    assert s % 128 == 0 and s >= WINDOW + 128 and b + nb <= 16

    w = _prep_weights(w_in, pool_map, pool_scale, cmp_pos, cmp_w, w_br_pool, w_br_nsa, w_out, w_up, w_down)
    c_all = jnp.zeros((16, d), F32).at[:b].set(c_prompt).at[b:b + nb].set(c_sample)
    mod = _modulation(c_all, ada_w, ada_b).reshape(depth, 16, 6, d)

    tm = min(1024, s)
    tab_p = _rope_tables(jnp.arange(s))
    tab_s = _rope_tables(past_len + jnp.arange(T_PAD))
    cache3 = cache_kv.reshape(depth, n_pool, page, 4 * KV_WIDTH)
    cache_win3 = cache_win.reshape(depth, nb, wb, 2 * KV_WIDTH)

    xp = x_prompt.reshape(b * s, d)
    xs = jnp.pad(x_sample, ((0, 0), (0, T_PAD - t_real), (0, 0))).reshape(nb * T_PAD, d)
    outs = [[] for _ in range(6)]
    for l in range(depth):
        modt_p = mod[l, :b][:, :, None, :]
        modt_s = jnp.repeat(mod[l, b:b + nb], T_PAD, axis=0).transpose(1, 0, 2)[None]
        xp, st_p = _layer_prompt(xp, modt_p, norm_g[l], w, l, tab_p, b, s, tm)
        xs, st_s = _layer_sample(xs, modt_s, norm_g[l], w, l, tab_s, cache3[l], cache_win3[l], state_pool[l],
                                 page_table, nb, t_real)
        for k in range(3):
            outs[2 * k].append(st_p[k])
            outs[2 * k + 1].append(st_s[k])
    y_p = xp.reshape(b, s, d)
    y_s = xs.reshape(nb, T_PAD, d)[:, :t_real]
    return (y_p, y_s) + tuple(jnp.stack(o) for o in outs)
```
